```python
import jax, jax.numpy as jnp
from jax import lax
import numpy as np

D_MODEL = 1024
BATCH = 8
SEQ = 2048
DEPTH = 1
DEC_BATCH = 128
DEC_SEQ = 1
PAST_LEN = 16384
PAGE_SIZE = 128

N_MEM = 256
D_MIX = D_MODEL
D_CONV = D_MIX // 2
D_GLA = D_MIX - D_CONV
CONV_W = 31
GLA_HEADS = 4
GLA_DK = D_GLA // 2 // GLA_HEADS
GLA_DV = D_GLA // GLA_HEADS
GLA_RANK = 16
GLA_TAU = 16.0
GLA_CHUNK = 16
XA_HEADS = 4
XA_HD = D_MODEL // XA_HEADS
D_FF = 4 * D_MODEL
EPS = 1e-6
D_IN = 2 * D_CONV + 2 * GLA_HEADS * GLA_DK + GLA_HEADS * GLA_DV + D_GLA + GLA_RANK

kernel_name = 'hybrid_conformer_gla_xattn_decoder_step'


def rms_norm(x, g):
    xf = x.astype(jnp.float32)
    y = xf * lax.rsqrt(jnp.mean(xf * xf, axis=-1, keepdims=True) + EPS)
    return (y * g.astype(jnp.float32)).astype(x.dtype)


def layer_norm(x, g, b):
    xf = x.astype(jnp.float32)
    mu = jnp.mean(xf, axis=-1, keepdims=True)
    xc = xf - mu
    y = xc * lax.rsqrt(jnp.mean(xc * xc, axis=-1, keepdims=True) + EPS)
    return (y * g.astype(jnp.float32) + b.astype(jnp.float32)).astype(x.dtype)


def conformer_conv(u, prefix, w_dw, b_dw, ln_g, ln_b):
    full = jnp.concatenate([prefix.astype(u.dtype), u], axis=1)
    y = lax.conv_general_dilated(
        full, w_dw.astype(u.dtype)[:, None, :], window_strides=(1,), padding='VALID',
        dimension_numbers=('NWC', 'WIO', 'NWC'), feature_group_count=D_CONV)
    y = jax.nn.silu(layer_norm(y + b_dw, ln_g, ln_b))
    return y, full[:, -(CONV_W - 1):]


def gla_chunked(q, k, v, log_a, s0):
    B, T = q.shape[0], q.shape[1]
    n = -(-T // GLA_CHUNK)
    pad = n * GLA_CHUNK - T

    def blocks(t):
        t = jnp.pad(t.astype(jnp.float32), ((0, 0), (0, pad), (0, 0), (0, 0)))
        return t.reshape(B, n, GLA_CHUNK, t.shape[2], t.shape[3])

    q, k, v, log_a = blocks(q), blocks(k), blocks(v), blocks(log_a)
    b = jnp.cumsum(log_a, axis=2)
    causal = jnp.tril(jnp.ones((GLA_CHUNK, GLA_CHUNK), dtype=bool))
    diff = b[:, :, :, None] - b[:, :, None]
    decay = jnp.exp(jnp.where(causal[:, :, None, None], diff, -jnp.inf))
    scores = jnp.einsum('bnihd,bnjhd,bnijhd->bnhij', q, k, decay)
    o_intra = jnp.einsum('bnhij,bnjhv->bnihv', scores, v)

    b_last = b[:, :, -1]
    q_in = q * jnp.exp(b)
    k_in = k * jnp.exp(b_last[:, :, None] - b)
    chunk_kv = jnp.einsum('bnjhd,bnjhv->bnhdv', k_in, v)

    def step(S, xs):
        qc, dec, kv = xs
        o = jnp.einsum('bihd,bhdv->bihv', qc, S)
        S = jnp.exp(dec)[..., None] * S + kv
        return S, o

    S, o_inter = lax.scan(step, s0.astype(jnp.float32),
                          (jnp.moveaxis(q_in, 1, 0), jnp.moveaxis(b_last, 1, 0),
                           jnp.moveaxis(chunk_kv, 1, 0)))
    o = o_intra + jnp.moveaxis(o_inter, 0, 1)
    o = o.reshape(B, n * GLA_CHUNK, o.shape[3], o.shape[4])[:, :T]
    return o, S


def token_mix(h, conv_prefix, s0, w_in, w_alpha, b_alpha, w_dw, b_dw, ln_g, ln_b,
              gla_g, w_out):
    B, T, _ = h.shape
    proj = h @ w_in
    i1 = D_CONV
    i2 = 2 * D_CONV
    i3 = i2 + GLA_HEADS * GLA_DK
    i4 = i3 + GLA_HEADS * GLA_DK
    i5 = i4 + GLA_HEADS * GLA_DV
    i6 = i5 + D_GLA
    c_val, c_gate, q, k, v, r, a_lr = jnp.split(proj, [i1, i2, i3, i4, i5, i6], axis=-1)
    u = c_val * jax.nn.sigmoid(c_gate)
    conv_out, conv_state = conformer_conv(u, conv_prefix, w_dw, b_dw, ln_g, ln_b)
    log_a = jax.nn.log_sigmoid((a_lr @ w_alpha + b_alpha).astype(jnp.float32)) / GLA_TAU
    o, s_new = gla_chunked(
        q.reshape(B, T, GLA_HEADS, GLA_DK) * (GLA_DK ** -0.5),
        k.reshape(B, T, GLA_HEADS, GLA_DK),
        v.reshape(B, T, GLA_HEADS, GLA_DV),
        log_a.reshape(B, T, GLA_HEADS, GLA_DK), s0)
    o = rms_norm(o, gla_g).astype(h.dtype).reshape(B, T, D_GLA) * jax.nn.silu(r)
    out = jnp.concatenate([conv_out, o], axis=-1) @ w_out
    return out, conv_state, s_new


def memory_kv(mem, g, w_k, w_v):
    B = mem.shape[0]
    m = rms_norm(mem, g)
    mk = (m @ w_k).reshape(B, N_MEM, XA_HEADS, XA_HD)
    mv = (m @ w_v).reshape(B, N_MEM, XA_HEADS, XA_HD)
    return mk, mv


def cross_attn(h, mk, mv, w_q, w_o):
    B, T, _ = h.shape
    q = (h @ w_q).reshape(B, T, XA_HEADS, XA_HD)
    s = jnp.einsum('bthd,bmhd->bhtm', q, mk).astype(jnp.float32) * (XA_HD ** -0.5)
    p = jax.nn.softmax(s, axis=-1).astype(mv.dtype)
    o = jnp.einsum('bhtm,bmhd->bthd', p, mv).reshape(B, T, D_MODEL)
    return o @ w_o


def setup_inputs(seed: int = 0) -> dict:
    key = jax.random.key(seed)
    ks = jax.random.split(key, 32)
    f32 = jnp.float32

    def nrm(k, shape, scale):
        return scale * jax.random.normal(k, shape, f32)

    return {
        'x_prompt': nrm(ks[0], (BATCH, SEQ, D_MODEL), 1.0),
        'x_sample': nrm(ks[1], (DEC_BATCH, DEC_SEQ, D_MODEL), 1.0),
        'cache_conv': nrm(ks[2], (DEPTH, DEC_BATCH, CONV_W - 1, D_CONV), 0.5),
        'state_gla': nrm(ks[3], (DEPTH, DEC_BATCH, GLA_HEADS, GLA_DK, GLA_DV), 1.0),
        'cache_mem_k': nrm(ks[4], (DEPTH, DEC_BATCH, N_MEM, XA_HEADS, XA_HD), 1.0),
        'cache_mem_v': nrm(ks[5], (DEPTH, DEC_BATCH, N_MEM, XA_HEADS, XA_HD), 1.0),
        'mem_prompt': nrm(ks[6], (BATCH, N_MEM, D_MODEL), 1.0),
        'norm_mix_g': 1.0 + nrm(ks[7], (DEPTH, D_MODEL), 0.02),
        'w_in': nrm(ks[8], (DEPTH, D_MODEL, D_IN), D_MODEL ** -0.5),
        'w_alpha': nrm(ks[9], (DEPTH, GLA_RANK, GLA_HEADS * GLA_DK), GLA_RANK ** -0.5),
        'b_alpha': nrm(ks[10], (DEPTH, GLA_HEADS * GLA_DK), 0.1),
        'w_dw': nrm(ks[11], (DEPTH, CONV_W, D_CONV), CONV_W ** -0.5),
        'b_dw': nrm(ks[12], (DEPTH, D_CONV), 0.02),
        'conv_ln_g': 1.0 + nrm(ks[13], (DEPTH, D_CONV), 0.02),
        'conv_ln_b': nrm(ks[14], (DEPTH, D_CONV), 0.02),
        'gla_norm_g': 1.0 + nrm(ks[15], (DEPTH, GLA_DV), 0.02),
        'w_out': nrm(ks[16], (DEPTH, D_MIX, D_MODEL), D_MIX ** -0.5),
        'norm_xa_g': 1.0 + nrm(ks[17], (DEPTH, D_MODEL), 0.02),
        'mem_norm_g': 1.0 + nrm(ks[18], (DEPTH, D_MODEL), 0.02),
        'w_xq': nrm(ks[19], (DEPTH, D_MODEL, D_MODEL), D_MODEL ** -0.5),
        'w_xk': nrm(ks[20], (DEPTH, D_MODEL, D_MODEL), D_MODEL ** -0.5),
        'w_xv': nrm(ks[21], (DEPTH, D_MODEL, D_MODEL), D_MODEL ** -0.5),
        'w_xo': nrm(ks[22], (DEPTH, D_MODEL, D_MODEL), D_MODEL ** -0.5),
        'norm_ffn_g': 1.0 + nrm(ks[23], (DEPTH, D_MODEL), 0.02),
        'w_up': nrm(ks[24], (DEPTH, D_MODEL, D_FF), D_MODEL ** -0.5),
        'w_down': nrm(ks[25], (DEPTH, D_FF, D_MODEL), D_FF ** -0.5),
        'final_g': 1.0 + nrm(ks[26], (D_MODEL,), 0.02),
    }


def reference(x_prompt, x_sample, cache_conv, state_gla, cache_mem_k, cache_mem_v,
              mem_prompt, norm_mix_g, w_in, w_alpha, b_alpha, w_dw, b_dw, conv_ln_g,
              conv_ln_b, gla_norm_g, w_out, norm_xa_g, mem_norm_g, w_xq, w_xk, w_xv,
              w_xo, norm_ffn_g, w_up, w_down, final_g):

    def run_layer(x, l, conv_prefix, s0, mk, mv):
        m, conv_state, s_new = token_mix(
            rms_norm(x, norm_mix_g[l]), conv_prefix, s0, w_in[l], w_alpha[l], b_alpha[l],
            w_dw[l], b_dw[l], conv_ln_g[l], conv_ln_b[l], gla_norm_g[l], w_out[l])
        x = x + m
        x = x + cross_attn(rms_norm(x, norm_xa_g[l]), mk, mv, w_xq[l], w_xo[l])
        h = rms_norm(x, norm_ffn_g[l])
        x = x + jnp.square(jax.nn.relu(h @ w_up[l])) @ w_down[l]
        return x, conv_state, s_new

    xp = x_prompt
    conv_p, gla_p, mk_p, mv_p = [], [], [], []
    for l in range(DEPTH):
        mk, mv = memory_kv(mem_prompt, mem_norm_g[l], w_xk[l], w_xv[l])
        prefix0 = jnp.zeros((xp.shape[0], CONV_W - 1, D_CONV), xp.dtype)
        s00 = jnp.zeros((xp.shape[0], GLA_HEADS, GLA_DK, GLA_DV), jnp.float32)
        xp, cs, ss = run_layer(xp, l, prefix0, s00, mk, mv)
        conv_p.append(cs)
        gla_p.append(ss)
        mk_p.append(mk)
        mv_p.append(mv)
    y_prompt = rms_norm(xp, final_g)

    xs = x_sample
    conv_s, gla_s = [], []
    for l in range(DEPTH):
        xs, cs, ss = run_layer(xs, l, cache_conv[l], state_gla[l], cache_mem_k[l], cache_mem_v[l])
        conv_s.append(cs)
        gla_s.append(ss)
    y_sample = rms_norm(xs, final_g)

    conv_prompt = jnp.stack(conv_p)
    conv_sample = jnp.stack(conv_s)
    gla_prompt = jnp.stack(gla_p)
    gla_sample = jnp.stack(gla_s)
    mem_k_prompt = jnp.stack(mk_p)
    mem_v_prompt = jnp.stack(mv_p)
    return (y_prompt, y_sample, conv_prompt, conv_sample, gla_prompt, gla_sample, mem_k_prompt, mem_v_prompt)
```

```python
import functools

import jax
import jax.numpy as jnp
from jax import lax
from jax.experimental import pallas as pl
from jax.experimental.pallas import tpu as pltpu

F32 = jnp.float32
BF16 = jnp.bfloat16

D_MODEL = 1024
D_CONV = 512
D_GLA = 512
CONV_W = 31
CONV_HIST = CONV_W - 1
GLA_HEADS = 4
GLA_DK = 64
GLA_DV = 128
GLA_QK = GLA_HEADS * GLA_DK
GLA_RANK = 16
GLA_TAU = 16.0
N_MEM = 256
XA_HEADS = 4
XA_HD = 256
D_FF = 4096
EPS = 1e-6

LANES = 128
SUBLANES = 8
VMEM_LIMIT_BYTES = 56 * 1024 * 1024

C_VAL = 0
C_GATE = D_CONV
C_Q = 2 * D_CONV
C_K = C_Q + GLA_QK
C_V = C_K + GLA_QK
C_R = C_V + D_GLA
C_A = C_R + D_GLA
D_IN = C_A + GLA_RANK
D_IN_PAD = C_A + LANES

CONV_PAD = 32
CONV_ROW_BLOCK = 64

MIX_TILE = 256
XA_TILE = 512
FFN_TILE = 512
FFN_CHUNK = 1024
SAMPLE_BLOCK = 32
STATE_BLOCK = 16
XA_SAMPLE_BLOCK = 4


def _const_spec(shape):
    n = len(shape)
    return pl.BlockSpec(shape, lambda *_: (0,) * n, pipeline_mode=pl.Buffered(1))


def _params(*sem):
    return pltpu.CompilerParams(dimension_semantics=sem, vmem_limit_bytes=VMEM_LIMIT_BYTES)


def _rms(x, g):
    return x * lax.rsqrt(jnp.mean(x * x, axis=-1, keepdims=True) + EPS) * g


def _layer_norm(x, g, b):
    mu = jnp.mean(x, axis=-1, keepdims=True)
    xc = x - mu
    return xc * lax.rsqrt(jnp.mean(xc * xc, axis=-1, keepdims=True) + EPS) * g + b


def _sigmoid(x):
    return 1.0 / (1.0 + jnp.exp(-x))


def _silu(x):
    return x * _sigmoid(x)


def _log_sigmoid(x):
    return jnp.minimum(x, 0.0) - jnp.log(1.0 + jnp.exp(-jnp.abs(x)))


def _dot(a, b):
    return jnp.dot(a, b, preferred_element_type=F32)


def _dot_nt(a, b):
    return lax.dot_general(a, b, (((1,), (1,)), ((), ())), preferred_element_type=F32)


def _dot_tn(a, b):
    return lax.dot_general(a, b, (((0,), (0,)), ((), ())), preferred_element_type=F32)


def _head_mask(width, per_head, h):
    lane = lax.broadcasted_iota(jnp.int32, (1, width), 1)
    return (lane >= h * per_head) & (lane < (h + 1) * per_head)


def _gla_out_norm(o, gla_g, r):
    parts = []
    for h in range(GLA_HEADS):
        parts.append(_rms(o[:, h * GLA_DV:(h + 1) * GLA_DV], gla_g))
    return jnp.concatenate(parts, axis=1) * _silu(r)


def _memkv_body(m_ref, g_ref, wk_ref, wv_ref, k_ref, v_ref):
    h = _rms(m_ref[...], g_ref[...]).astype(BF16)
    k_ref[...] = _dot(h, wk_ref[...])
    v_ref[...] = _dot(h, wv_ref[...])


def _memkv(mem, g, wk, wv):
    n = mem.shape[0]
    tile = 512
    row = pl.BlockSpec((tile, D_MODEL), lambda i: (i, 0))
    return pl.pallas_call(
        _memkv_body,
        grid=(n // tile,),
        in_specs=[row, _const_spec((1, D_MODEL)), _const_spec((D_MODEL, D_MODEL)),
                  _const_spec((D_MODEL, D_MODEL))],
        out_specs=[row, row],
        out_shape=[jax.ShapeDtypeStruct((n, D_MODEL), F32)] * 2,
        compiler_params=_params("arbitrary"),
        name="memkv",
    )(mem, g, wk, wv)


def _mix_prompt_body(x_ref, g_ref, win_ref, walpha_ref, balpha_ref, wdw_ref, bdw_ref,
                     lng_ref, lnb_ref, glag_ref, wout_ref,
                     y_ref, conv_ref, gla_ref,
                     ubuf, st_ref, conv_out):
    t = pl.program_id(1)
    n_t = pl.num_programs(1)
    tile = x_ref.shape[1]

    @pl.when(t == 0)
    def _():
        ubuf[0:CONV_PAD, :] = jnp.zeros((CONV_PAD, D_CONV), F32)
        st_ref[...] = jnp.zeros_like(st_ref)

    x = x_ref[0]
    h = _rms(x, g_ref[...]).astype(BF16)
    proj = _dot(h, win_ref[...])

    u = proj[:, C_VAL:C_VAL + D_CONV] * _sigmoid(proj[:, C_GATE:C_GATE + D_CONV])
    ubuf[CONV_PAD:CONV_PAD + tile, :] = u
    first = CONV_PAD - CONV_HIST
    for c in range(D_CONV // LANES):
        cs = slice(c * LANES, (c + 1) * LANES)
        for r in range(tile // CONV_ROW_BLOCK):
            acc = jnp.zeros((CONV_ROW_BLOCK, LANES), F32)
            for k in range(CONV_W):
                lo = first + k + r * CONV_ROW_BLOCK
                acc = acc + ubuf[lo:lo + CONV_ROW_BLOCK, cs] * wdw_ref[k:k + 1, cs]
            conv_out[r * CONV_ROW_BLOCK:(r + 1) * CONV_ROW_BLOCK, cs] = acc
    conv = _silu(_layer_norm(conv_out[...] + bdw_ref[...], lng_ref[...], lnb_ref[...]))

    @pl.when(t == n_t - 1)
    def _():
        conv_ref[0] = ubuf[CONV_PAD + tile - CONV_HIST:CONV_PAD + tile, :]

    ubuf[0:CONV_PAD, :] = ubuf[tile:tile + CONV_PAD, :]

    z = _dot(proj[:, C_A:C_A + LANES].astype(BF16), walpha_ref[...]) + balpha_ref[...]
    la = _log_sigmoid(z) / GLA_TAU
    la_hi = la.astype(BF16)
    rem = la - la_hi.astype(F32)
    la_mid = rem.astype(BF16)
    la_lo = (rem - la_mid.astype(F32)).astype(BF16)
    row = lax.broadcasted_iota(jnp.int32, (tile, tile), 0)
    col = lax.broadcasted_iota(jnp.int32, (tile, tile), 1)
    causal = row >= col
    tril = jnp.where(causal, 1.0, 0.0).astype(BF16)
    csum = _dot(tril, jnp.concatenate([la_hi, la_mid, la_lo], axis=1))
    b = csum[:, 0:GLA_QK] + csum[:, GLA_QK:2 * GLA_QK] + csum[:, 2 * GLA_QK:3 * GLA_QK]
    b_last = b[tile - 1:tile, :]

    q = proj[:, C_Q:C_Q + GLA_QK] * (GLA_DK ** -0.5)
    k = proj[:, C_K:C_K + GLA_QK]
    v = proj[:, C_V:C_V + D_GLA].astype(BF16)
    qe = q * jnp.exp(b)
    ke = (k * jnp.exp(-b)).astype(BF16)
    kl = (k * jnp.exp(b_last - b)).astype(BF16)
    qe4 = jnp.concatenate(
        [jnp.where(_head_mask(GLA_QK, GLA_DK, hh), qe, 0.0) for hh in range(GLA_HEADS)],
        axis=0).astype(BF16)
    scores = _dot_nt(qe4, ke)
    st = st_ref[...]
    inter = _dot_nt(qe4, st.astype(BF16))
    outs = []
    for hh in range(GLA_HEADS):
        p = jnp.where(causal, scores[hh * tile:(hh + 1) * tile, :], 0.0).astype(BF16)
        outs.append(_dot(p, v[:, hh * GLA_DV:(hh + 1) * GLA_DV])
                    + inter[hh * tile:(hh + 1) * tile, :])
    o = jnp.concatenate(outs, axis=1)

    upd = _dot_tn(v, kl)
    new_st = st * jnp.exp(b_last)
    for hh in range(GLA_HEADS):
        new_st = new_st + jnp.where(_head_mask(GLA_QK, GLA_DK, hh),
                                    upd[hh * GLA_DV:(hh + 1) * GLA_DV, :], 0.0)
    st_ref[...] = new_st

    @pl.when(t == n_t - 1)
    def _():
        gla_ref[0] = jnp.transpose(new_st)

    og = _gla_out_norm(o, glag_ref[...], proj[:, C_R:C_R + D_GLA])
    mixed = jnp.concatenate([conv, og], axis=1).astype(BF16)
    y_ref[0] = x + _dot(mixed, wout_ref[...])


def _mix_prompt(x, g, win, walpha, balpha, wdw, bdw, lng, lnb, glag, wout):
    bsz, seq, _ = x.shape
    tile = MIX_TILE
    xspec = pl.BlockSpec((1, tile, D_MODEL), lambda b, t: (b, t, 0))
    return pl.pallas_call(
        _mix_prompt_body,
        grid=(bsz, seq // tile),
        in_specs=[xspec,
                  _const_spec((1, D_MODEL)),
                  _const_spec((D_MODEL, D_IN_PAD)),
                  _const_spec((LANES, GLA_QK)),
                  _const_spec((1, GLA_QK)),
                  _const_spec((CONV_W, D_CONV)),
                  _const_spec((1, D_CONV)),
                  _const_spec((1, D_CONV)),
                  _const_spec((1, D_CONV)),
                  _const_spec((1, GLA_DV)),
                  _const_spec((D_MODEL, D_MODEL))],
        out_specs=[xspec,
                   pl.BlockSpec((1, CONV_HIST, D_CONV), lambda b, t: (b, 0, 0)),
                   pl.BlockSpec((1, GLA_QK, GLA_DV), lambda b, t: (b, 0, 0))],
        out_shape=[jax.ShapeDtypeStruct((bsz, seq, D_MODEL), F32),
                   jax.ShapeDtypeStruct((bsz, CONV_HIST, D_CONV), F32),
                   jax.ShapeDtypeStruct((bsz, GLA_QK, GLA_DV), F32)],
        scratch_shapes=[pltpu.VMEM((CONV_PAD + tile, D_CONV), F32),
                        pltpu.VMEM((GLA_DV, GLA_QK), F32),
                        pltpu.VMEM((tile, D_CONV), F32)],
        compiler_params=_params("arbitrary", "arbitrary"),
        name="mix_prompt",
    )(x, g, win, walpha, balpha, wdw, bdw, lng, lnb, glag, wout)


def _softmax_rows(s):
    m = jnp.max(s, axis=-1, keepdims=True)
    p = jnp.exp(s - m)
    return p / jnp.sum(p, axis=-1, keepdims=True)


def _xattn_prompt_body(x_ref, mk_ref, mv_ref, g_ref, wq_ref, wo_ref, y_ref):
    x = x_ref[0]
    h = _rms(x, g_ref[...]).astype(BF16)
    q = (_dot(h, wq_ref[...]) * (XA_HD ** -0.5)).astype(BF16)
    outs = []
    for hh in range(XA_HEADS):
        hs = slice(hh * XA_HD, (hh + 1) * XA_HD)
        s = _dot_nt(q[:, hs], mk_ref[0, :, hs].astype(BF16))
        p = _softmax_rows(s).astype(BF16)
        outs.append(_dot(p, mv_ref[0, :, hs].astype(BF16)))
    o = jnp.concatenate(outs, axis=1).astype(BF16)
    y_ref[0] = x + _dot(o, wo_ref[...])


def _xattn_prompt(x, mk, mv, g, wq, wo):
    bsz, seq, _ = x.shape
    tile = XA_TILE
    xspec = pl.BlockSpec((1, tile, D_MODEL), lambda b, t: (b, t, 0))
    mspec = pl.BlockSpec((1, N_MEM, D_MODEL), lambda b, t: (b, 0, 0))
    return pl.pallas_call(
        _xattn_prompt_body,
        grid=(bsz, seq // tile),
        in_specs=[xspec, mspec, mspec, _const_spec((1, D_MODEL)),
                  _const_spec((D_MODEL, D_MODEL)), _const_spec((D_MODEL, D_MODEL))],
        out_specs=xspec,
        out_shape=jax.ShapeDtypeStruct((bsz, seq, D_MODEL), F32),
        compiler_params=_params("arbitrary", "arbitrary"),
        name="xattn_prompt",
    )(x, mk, mv, g, wq, wo)


def _ffn_body(x_ref, g_ref, wup_ref, wdn_ref, gf_ref, y_ref):
    x = x_ref[...]
    h = _rms(x, g_ref[...]).astype(BF16)
    acc = x
    for c in range(D_FF // FFN_CHUNK):
        cs = slice(c * FFN_CHUNK, (c + 1) * FFN_CHUNK)
        a = jnp.square(jnp.maximum(_dot(h, wup_ref[:, cs]), 0.0)).astype(BF16)
        acc = acc + _dot(a, wdn_ref[cs, :])
    y_ref[...] = _rms(acc, gf_ref[...])


def _ffn(x, g, wup, wdn, gf):
    n = x.shape[0]
    tile = min(FFN_TILE, n)
    row = pl.BlockSpec((tile, D_MODEL), lambda i: (i, 0))
    return pl.pallas_call(
        _ffn_body,
        grid=(n // tile,),
        in_specs=[row, _const_spec((1, D_MODEL)), _const_spec((D_MODEL, D_FF)),
                  _const_spec((D_FF, D_MODEL)), _const_spec((1, D_MODEL))],
        out_specs=row,
        out_shape=jax.ShapeDtypeStruct((n, D_MODEL), F32),
        compiler_params=_params("arbitrary"),
        name="ffn",
    )(x, g, wup, wdn, gf)


def _sample_in_body(x_ref, cache_ref, g_ref, win_ref, walpha_ref, balpha_ref, wdw_ref,
                    bdw_ref, lng_ref, lnb_ref,
                    conv_ref, cache_out_ref, a_ref, k_ref, q_ref, v_ref, r_ref):
    x = x_ref[...]
    h = _rms(x, g_ref[...]).astype(BF16)
    proj = _dot(h, win_ref[...])
    u = proj[:, C_VAL:C_VAL + D_CONV] * _sigmoid(proj[:, C_GATE:C_GATE + D_CONV])
    cache = cache_ref[...]
    w = wdw_ref[...]
    y = jnp.sum(cache * w[0:CONV_HIST, :][None], axis=1) + u * w[CONV_HIST:CONV_W, :]
    conv_ref[...] = _silu(_layer_norm(y + bdw_ref[...], lng_ref[...], lnb_ref[...]))
    cache_out_ref[:, 0:CONV_HIST - 1, :] = cache[:, 1:CONV_HIST, :]
    cache_out_ref[:, CONV_HIST - 1:CONV_HIST, :] = u[:, None, :]

    z = _dot(proj[:, C_A:C_A + LANES].astype(BF16), walpha_ref[...]) + balpha_ref[...]
    a_ref[...] = jnp.exp(_log_sigmoid(z) / GLA_TAU)
    k_ref[...] = proj[:, C_K:C_K + GLA_QK]
    q_ref[...] = proj[:, C_Q:C_Q + GLA_QK] * (GLA_DK ** -0.5)
    v_ref[...] = proj[:, C_V:C_V + D_GLA]
    r_ref[...] = proj[:, C_R:C_R + D_GLA]


def _sample_in(x, cache, g, win, walpha, balpha, wdw, bdw, lng, lnb):
    n = x.shape[0]
    bs = SAMPLE_BLOCK

    def rows(width):
        return pl.BlockSpec((bs, width), lambda i: (i, 0))

    cspec = pl.BlockSpec((bs, CONV_HIST, D_CONV), lambda i: (i, 0, 0))
    return pl.pallas_call(
        _sample_in_body,
        grid=(n // bs,),
        in_specs=[rows(D_MODEL), cspec,
                  _const_spec((1, D_MODEL)),
                  _const_spec((D_MODEL, D_IN_PAD)),
                  _const_spec((LANES, GLA_QK)),
                  _const_spec((1, GLA_QK)),
                  _const_spec((CONV_W, D_CONV)),
                  _const_spec((1, D_CONV)),
                  _const_spec((1, D_CONV)),
                  _const_spec((1, D_CONV))],
        out_specs=[rows(D_CONV), cspec, rows(GLA_QK), rows(GLA_QK), rows(GLA_QK),
                   rows(D_GLA), rows(D_GLA)],
        out_shape=[jax.ShapeDtypeStruct((n, D_CONV), F32),
                   jax.ShapeDtypeStruct((n, CONV_HIST, D_CONV), F32),
                   jax.ShapeDtypeStruct((n, GLA_QK), F32),
                   jax.ShapeDtypeStruct((n, GLA_QK), F32),
                   jax.ShapeDtypeStruct((n, GLA_QK), F32),
                   jax.ShapeDtypeStruct((n, D_GLA), F32),
                   jax.ShapeDtypeStruct((n, D_GLA), F32)],
        compiler_params=_params("arbitrary"),
        name="sample_in",
    )(x, cache, g, win, walpha, balpha, wdw, bdw, lng, lnb)


def _sample_state_body(s_ref, at_ref, kt_ref, qt_ref, v_ref, snew_ref, o_ref):
    bs = s_ref.shape[0]
    at = at_ref[0]
    kt = kt_ref[0]
    qt = qt_ref[0]
    for i in range(bs):
        a_col = at[:, i:i + 1]
        k_col = kt[:, i:i + 1]
        q_col = qt[:, i:i + 1]
        for hh in range(GLA_HEADS):
            ds = slice(hh * GLA_DK, (hh + 1) * GLA_DK)
            vs = slice(hh * GLA_DV, (hh + 1) * GLA_DV)
            s_new = a_col[ds, :] * s_ref[i, hh] + k_col[ds, :] * v_ref[i:i + 1, vs]
            snew_ref[i, hh] = s_new
            o_ref[i:i + 1, vs] = jnp.sum(q_col[ds, :] * s_new, axis=0, keepdims=True)


def _sample_state(state, at, kt, qt, v):
    n = state.shape[0]
    bs = STATE_BLOCK
    sspec = pl.BlockSpec((bs, GLA_HEADS, GLA_DK, GLA_DV), lambda i: (i, 0, 0, 0))
    tspec = pl.BlockSpec((1, GLA_QK, bs), lambda i: (i, 0, 0))
    vspec = pl.BlockSpec((bs, D_GLA), lambda i: (i, 0))
    return pl.pallas_call(
        _sample_state_body,
        grid=(n // bs,),
        in_specs=[sspec, tspec, tspec, tspec, vspec],
        out_specs=[sspec, vspec],
        out_shape=[jax.ShapeDtypeStruct(state.shape, F32),
                   jax.ShapeDtypeStruct((n, D_GLA), F32)],
        compiler_params=_params("arbitrary"),
        name="sample_state",
    )(state, at, kt, qt, v)


def _sample_mid_body(x_ref, conv_ref, o_ref, r_ref, glag_ref, wout_ref, gx_ref, wq_ref,
                     x1_ref, q_ref):
    og = _gla_out_norm(o_ref[...], glag_ref[...], r_ref[...])
    mixed = jnp.concatenate([conv_ref[...], og], axis=1).astype(BF16)
    x1 = x_ref[...] + _dot(mixed, wout_ref[...])
    x1_ref[...] = x1
    h = _rms(x1, gx_ref[...]).astype(BF16)
    q_ref[...] = _dot(h, wq_ref[...]) * (XA_HD ** -0.5)


def _sample_mid(x, conv, o, r, glag, wout, gx, wq):
    n = x.shape[0]
    full = lambda w: pl.BlockSpec((n, w), lambda i: (0, 0))
    return pl.pallas_call(
        _sample_mid_body,
        grid=(1,),
        in_specs=[full(D_MODEL), full(D_CONV), full(D_GLA), full(D_GLA),
                  _const_spec((1, GLA_DV)), _const_spec((D_MODEL, D_MODEL)),
                  _const_spec((1, D_MODEL)), _const_spec((D_MODEL, D_MODEL))],
        out_specs=[full(D_MODEL), full(D_MODEL)],
        out_shape=[jax.ShapeDtypeStruct((n, D_MODEL), F32)] * 2,
        compiler_params=_params("arbitrary"),
        name="sample_mid",
    )(x, conv, o, r, glag, wout, gx, wq)


def _xattn_sample_body(q_ref, k_ref, v_ref, o_ref):
    bs = k_ref.shape[0]
    rowi = lax.broadcasted_iota(jnp.int32, (SUBLANES, D_MODEL), 0)
    lane = lax.broadcasted_iota(jnp.int32, (SUBLANES, D_MODEL), 1)
    own = (lane >= rowi * XA_HD) & (lane < (rowi + 1) * XA_HD)
    for i in range(bs):
        q = q_ref[0, i:i + 1, :]
        qrows = jnp.where(own, q, 0.0).astype(BF16)
        s = _dot_nt(qrows, k_ref[i].astype(BF16))
        p = _softmax_rows(s).astype(BF16)
        full = _dot(p, v_ref[i].astype(BF16))
        o_ref[0, i:i + 1, :] = jnp.sum(jnp.where(own, full, 0.0), axis=0, keepdims=True)


def _xattn_sample(q, mk, mv):
    n = mk.shape[0]
    bs = XA_SAMPLE_BLOCK
    qspec = pl.BlockSpec((1, bs, D_MODEL), lambda i: (i, 0, 0))
    mspec = pl.BlockSpec((bs, N_MEM, D_MODEL), lambda i: (i, 0, 0))
    out = pl.pallas_call(
        _xattn_sample_body,
        grid=(n // bs,),
        in_specs=[qspec, mspec, mspec],
        out_specs=qspec,
        out_shape=jax.ShapeDtypeStruct((n // bs, bs, D_MODEL), F32),
        compiler_params=_params("arbitrary"),
        name="xattn_sample",
    )(q.reshape(n // bs, bs, D_MODEL), mk, mv)
    return out.reshape(n, D_MODEL)


def _sample_xo_body(x_ref, o_ref, wo_ref, y_ref):
    y_ref[...] = x_ref[...] + _dot(o_ref[...].astype(BF16), wo_ref[...])


def _sample_xo(x, o, wo):
    n = x.shape[0]
    full = pl.BlockSpec((n, D_MODEL), lambda i: (0, 0))
    return pl.pallas_call(
        _sample_xo_body,
        grid=(1,),
        in_specs=[full, full, _const_spec((D_MODEL, D_MODEL))],
        out_specs=full,
        out_shape=jax.ShapeDtypeStruct((n, D_MODEL), F32),
        compiler_params=_params("arbitrary"),
        name="sample_xo",
    )(x, o, wo)


def _row(v):
    return v.reshape(1, -1)


def _layer(l, x_prompt, x_sample, cache_conv, state_gla, cache_mem_k, cache_mem_v, mem_prompt,
           norm_mix_g, w_in, w_alpha, b_alpha, w_dw, b_dw, conv_ln_g, conv_ln_b, gla_norm_g,
           w_out, norm_xa_g, mem_norm_g, w_xq, w_xk, w_xv, w_xo, norm_ffn_g, w_up, w_down,
           final_g):
    bsz, seq, _ = x_prompt.shape
    n_s = x_sample.shape[0]

    win = jnp.pad(w_in[l], ((0, 0), (0, D_IN_PAD - D_IN))).astype(BF16)
    walpha = jnp.pad(w_alpha[l], ((0, LANES - GLA_RANK), (0, 0))).astype(BF16)
    balpha = _row(b_alpha[l])
    wout = w_out[l].astype(BF16)
    wq, wk, wv, wo = (w.astype(BF16) for w in (w_xq[l], w_xk[l], w_xv[l], w_xo[l]))
    wup = w_up[l].astype(BF16)
    wdn = w_down[l].astype(BF16)
    gm, gx, gmem, gf_, gfin = (_row(v) for v in (norm_mix_g[l], norm_xa_g[l], mem_norm_g[l],
                                                 norm_ffn_g[l], final_g))
    bdw, lng, lnb, glag = (_row(v) for v in (b_dw[l], conv_ln_g[l], conv_ln_b[l], gla_norm_g[l]))
    wdw = w_dw[l]

    mk, mv = _memkv(mem_prompt.reshape(bsz * N_MEM, D_MODEL), gmem, wk, wv)
    mk = mk.reshape(bsz, N_MEM, D_MODEL)
    mv = mv.reshape(bsz, N_MEM, D_MODEL)
    x1, conv_p, gla_p = _mix_prompt(x_prompt, gm, win, walpha, balpha, wdw, bdw, lng, lnb,
                                    glag, wout)
    x2 = _xattn_prompt(x1, mk, mv, gx, wq, wo)
    y_p = _ffn(x2.reshape(bsz * seq, D_MODEL), gf_, wup, wdn, gfin).reshape(bsz, seq, D_MODEL)

    xs = x_sample.reshape(n_s, D_MODEL)
    conv_s_out, cache_new, a, k, q, v, r = _sample_in(
        xs, cache_conv[l], gm, win, walpha, balpha, wdw, bdw, lng, lnb)
    nb = n_s // STATE_BLOCK

    def cols(m):
        return m.reshape(nb, STATE_BLOCK, GLA_QK).transpose(0, 2, 1)

    state_new, o_s = _sample_state(state_gla[l], cols(a), cols(k), cols(q), v)
    x1s, qx = _sample_mid(xs, conv_s_out, o_s, r, glag, wout, gx, wq)
    oa = _xattn_sample(qx,
                       cache_mem_k[l].reshape(n_s, N_MEM, D_MODEL),
                       cache_mem_v[l].reshape(n_s, N_MEM, D_MODEL))
    x2s = _sample_xo(x1s, oa, wo)
    y_s = _ffn(x2s, gf_, wup, wdn, gfin).reshape(n_s, 1, D_MODEL)

    return (y_p, y_s, conv_p, cache_new,
            gla_p.reshape(bsz, GLA_HEADS, GLA_DK, GLA_DV), state_new,
            mk.reshape(bsz, N_MEM, XA_HEADS, XA_HD), mv.reshape(bsz, N_MEM, XA_HEADS, XA_HD))


def kernel(x_prompt, x_sample, cache_conv, state_gla, cache_mem_k, cache_mem_v, mem_prompt,
           norm_mix_g, w_in, w_alpha, b_alpha, w_dw, b_dw, conv_ln_g, conv_ln_b, gla_norm_g,
           w_out, norm_xa_g, mem_norm_g, w_xq, w_xk, w_xv, w_xo, norm_ffn_g, w_up, w_down,
           final_g):
    assert w_in.shape[0] == 1, "single-layer trunk"
    outs = _layer(0, x_prompt, x_sample, cache_conv, state_gla, cache_mem_k, cache_mem_v,
                  mem_prompt, norm_mix_g, w_in, w_alpha, b_alpha, w_dw, b_dw, conv_ln_g,
                  conv_ln_b, gla_norm_g, w_out, norm_xa_g, mem_norm_g, w_xq, w_xk, w_xv, w_xo,
                  norm_ffn_g, w_up, w_down, final_g)
    y_p, y_s, conv_p, conv_s, gla_p, gla_s, mk, mv = outs
    return (y_p, y_s, conv_p[None], conv_s[None], gla_p[None], gla_s[None], mk[None], mv[None])
```

```python
import functools

import jax
import jax.numpy as jnp
from jax import lax
from jax.experimental import pallas as pl
from jax.experimental.pallas import tpu as pltpu

F32 = jnp.float32
BF16 = jnp.bfloat16

D_MODEL = 1024
D_CONV = 512
D_GLA = 512
CONV_W = 31
CONV_HIST = CONV_W - 1
GLA_HEADS = 4
GLA_DK = 64
GLA_DV = 128
GLA_QK = GLA_HEADS * GLA_DK
GLA_RANK = 16
GLA_TAU = 16.0
N_MEM = 256
XA_HEADS = 4
XA_HD = 256
D_FF = 4096
EPS = 1e-6

LANES = 128
SUBLANES = 8
VMEM_LIMIT_BYTES = 56 * 1024 * 1024

C_VAL = 0
C_GATE = D_CONV
C_Q = 2 * D_CONV
C_K = C_Q + GLA_QK
C_V = C_K + GLA_QK
C_R = C_V + D_GLA
C_A = C_R + D_GLA
D_IN = C_A + GLA_RANK
D_IN_PAD = C_A + LANES

CONV_PAD = 32
CONV_ROW_BLOCK = 64

MIX_TILE = 256
XA_TILE = 512
FFN_TILE = 512
FFN_CHUNK = 1024
SAMPLE_BLOCK = 32
STATE_BLOCK = 16
XA_SAMPLE_BLOCK = 8

XA_LANE_TILES = XA_HD // LANES
XA_ROWS = XA_HEADS * XA_LANE_TILES


def _xa_col(row):
    return (row % XA_HEADS) * XA_HD + (row // XA_HEADS) * LANES


def _const_spec(shape):
    n = len(shape)
    return pl.BlockSpec(shape, lambda *_: (0,) * n, pipeline_mode=pl.Buffered(1))


def _params(*sem):
    return pltpu.CompilerParams(dimension_semantics=sem, vmem_limit_bytes=VMEM_LIMIT_BYTES)


def _rms(x, g):
    return x * lax.rsqrt(jnp.mean(x * x, axis=-1, keepdims=True) + EPS) * g


def _layer_norm(x, g, b):
    mu = jnp.mean(x, axis=-1, keepdims=True)
    xc = x - mu
    return xc * lax.rsqrt(jnp.mean(xc * xc, axis=-1, keepdims=True) + EPS) * g + b


def _sigmoid(x):
    return 1.0 / (1.0 + jnp.exp(-x))


def _silu(x):
    return x * _sigmoid(x)


def _log_sigmoid(x):
    return jnp.minimum(x, 0.0) - jnp.log(1.0 + jnp.exp(-jnp.abs(x)))


def _dot(a, b):
    return jnp.dot(a, b, preferred_element_type=F32)


def _dot_nt(a, b):
    return lax.dot_general(a, b, (((1,), (1,)), ((), ())), preferred_element_type=F32)


def _dot_tn(a, b):
    return lax.dot_general(a, b, (((0,), (0,)), ((), ())), preferred_element_type=F32)


def _head_mask(width, per_head, h):
    lane = lax.broadcasted_iota(jnp.int32, (1, width), 1)
    return (lane >= h * per_head) & (lane < (h + 1) * per_head)


def _gla_out_norm(o, gla_g, r):
    parts = []
    for h in range(GLA_HEADS):
        parts.append(_rms(o[:, h * GLA_DV:(h + 1) * GLA_DV], gla_g))
    return jnp.concatenate(parts, axis=1) * _silu(r)


def _memkv_body(m_ref, g_ref, wk_ref, wv_ref, k_ref, v_ref):
    h = _rms(m_ref[...], g_ref[...]).astype(BF16)
    k_ref[...] = _dot(h, wk_ref[...])
    v_ref[...] = _dot(h, wv_ref[...])


def _memkv(mem, g, wk, wv):
    n = mem.shape[0]
    tile = 512
    row = pl.BlockSpec((tile, D_MODEL), lambda i: (i, 0))
    return pl.pallas_call(
        _memkv_body,
        grid=(n // tile,),
        in_specs=[row, _const_spec((1, D_MODEL)), _const_spec((D_MODEL, D_MODEL)),
                  _const_spec((D_MODEL, D_MODEL))],
        out_specs=[row, row],
        out_shape=[jax.ShapeDtypeStruct((n, D_MODEL), F32)] * 2,
        compiler_params=_params("arbitrary"),
        name="memkv",
    )(mem, g, wk, wv)


def _mix_prompt_body(x_ref, g_ref, win_ref, walpha_ref, balpha_ref, wdw_ref, bdw_ref,
                     lng_ref, lnb_ref, glag_ref, wout_ref,
                     y_ref, conv_ref, gla_ref,
                     ubuf, st_ref, conv_out):
    t = pl.program_id(1)
    n_t = pl.num_programs(1)
    tile = x_ref.shape[1]

    @pl.when(t == 0)
    def _():
        ubuf[0:CONV_PAD, :] = jnp.zeros((CONV_PAD, D_CONV), F32)
        st_ref[...] = jnp.zeros_like(st_ref)

    x = x_ref[0]
    h = _rms(x, g_ref[...]).astype(BF16)
    proj = _dot(h, win_ref[...])

    u = proj[:, C_VAL:C_VAL + D_CONV] * _sigmoid(proj[:, C_GATE:C_GATE + D_CONV])
    ubuf[CONV_PAD:CONV_PAD + tile, :] = u
    first = CONV_PAD - CONV_HIST
    for c in range(D_CONV // LANES):
        cs = slice(c * LANES, (c + 1) * LANES)
        for r in range(tile // CONV_ROW_BLOCK):
            acc = jnp.zeros((CONV_ROW_BLOCK, LANES), F32)
            for k in range(CONV_W):
                lo = first + k + r * CONV_ROW_BLOCK
                acc = acc + ubuf[lo:lo + CONV_ROW_BLOCK, cs] * wdw_ref[k:k + 1, cs]
            conv_out[r * CONV_ROW_BLOCK:(r + 1) * CONV_ROW_BLOCK, cs] = acc
    conv = _silu(_layer_norm(conv_out[...] + bdw_ref[...], lng_ref[...], lnb_ref[...]))

    @pl.when(t == n_t - 1)
    def _():
        conv_ref[0] = ubuf[CONV_PAD + tile - CONV_HIST:CONV_PAD + tile, :]

    ubuf[0:CONV_PAD, :] = ubuf[tile:tile + CONV_PAD, :]

    z = _dot(proj[:, C_A:C_A + LANES].astype(BF16), walpha_ref[...]) + balpha_ref[...]
    la = _log_sigmoid(z) / GLA_TAU
    la_hi = la.astype(BF16)
    rem = la - la_hi.astype(F32)
    la_mid = rem.astype(BF16)
    la_lo = (rem - la_mid.astype(F32)).astype(BF16)
    row = lax.broadcasted_iota(jnp.int32, (tile, tile), 0)
    col = lax.broadcasted_iota(jnp.int32, (tile, tile), 1)
    causal = row >= col
    tril = jnp.where(causal, 1.0, 0.0).astype(BF16)
    csum = _dot(tril, jnp.concatenate([la_hi, la_mid, la_lo], axis=1))
    b = csum[:, 0:GLA_QK] + csum[:, GLA_QK:2 * GLA_QK] + csum[:, 2 * GLA_QK:3 * GLA_QK]
    b_last = b[tile - 1:tile, :]

    q = proj[:, C_Q:C_Q + GLA_QK] * (GLA_DK ** -0.5)
    k = proj[:, C_K:C_K + GLA_QK]
    v = proj[:, C_V:C_V + D_GLA].astype(BF16)
    qe = q * jnp.exp(b)
    ke = (k * jnp.exp(-b)).astype(BF16)
    kl = (k * jnp.exp(b_last - b)).astype(BF16)
    qe4 = jnp.concatenate(
        [jnp.where(_head_mask(GLA_QK, GLA_DK, hh), qe, 0.0) for hh in range(GLA_HEADS)],
        axis=0).astype(BF16)
    scores = _dot_nt(qe4, ke)
    st = st_ref[...]
    inter = _dot_nt(qe4, st.astype(BF16))
    outs = []
    for hh in range(GLA_HEADS):
        p = jnp.where(causal, scores[hh * tile:(hh + 1) * tile, :], 0.0).astype(BF16)
        outs.append(_dot(p, v[:, hh * GLA_DV:(hh + 1) * GLA_DV])
                    + inter[hh * tile:(hh + 1) * tile, :])
    o = jnp.concatenate(outs, axis=1)

    upd = _dot_tn(v, kl)
    new_st = st * jnp.exp(b_last)
    for hh in range(GLA_HEADS):
        new_st = new_st + jnp.where(_head_mask(GLA_QK, GLA_DK, hh),
                                    upd[hh * GLA_DV:(hh + 1) * GLA_DV, :], 0.0)
    st_ref[...] = new_st

    @pl.when(t == n_t - 1)
    def _():
        gla_ref[0] = jnp.transpose(new_st)

    og = _gla_out_norm(o, glag_ref[...], proj[:, C_R:C_R + D_GLA])
    mixed = jnp.concatenate([conv, og], axis=1).astype(BF16)
    y_ref[0] = x + _dot(mixed, wout_ref[...])


def _mix_prompt(x, g, win, walpha, balpha, wdw, bdw, lng, lnb, glag, wout):
    bsz, seq, _ = x.shape
    tile = MIX_TILE
    xspec = pl.BlockSpec((1, tile, D_MODEL), lambda b, t: (b, t, 0))
    return pl.pallas_call(
        _mix_prompt_body,
        grid=(bsz, seq // tile),
        in_specs=[xspec,
                  _const_spec((1, D_MODEL)),
                  _const_spec((D_MODEL, D_IN_PAD)),
                  _const_spec((LANES, GLA_QK)),
                  _const_spec((1, GLA_QK)),
                  _const_spec((CONV_W, D_CONV)),
                  _const_spec((1, D_CONV)),
                  _const_spec((1, D_CONV)),
                  _const_spec((1, D_CONV)),
                  _const_spec((1, GLA_DV)),
                  _const_spec((D_MODEL, D_MODEL))],
        out_specs=[xspec,
                   pl.BlockSpec((1, CONV_HIST, D_CONV), lambda b, t: (b, 0, 0)),
                   pl.BlockSpec((1, GLA_QK, GLA_DV), lambda b, t: (b, 0, 0))],
        out_shape=[jax.ShapeDtypeStruct((bsz, seq, D_MODEL), F32),
                   jax.ShapeDtypeStruct((bsz, CONV_HIST, D_CONV), F32),
                   jax.ShapeDtypeStruct((bsz, GLA_QK, GLA_DV), F32)],
        scratch_shapes=[pltpu.VMEM((CONV_PAD + tile, D_CONV), F32),
                        pltpu.VMEM((GLA_DV, GLA_QK), F32),
                        pltpu.VMEM((tile, D_CONV), F32)],
        compiler_params=_params("arbitrary", "arbitrary"),
        name="mix_prompt",
    )(x, g, win, walpha, balpha, wdw, bdw, lng, lnb, glag, wout)


def _softmax_rows(s):
    m = jnp.max(s, axis=-1, keepdims=True)
    p = jnp.exp(s - m)
    return p / jnp.sum(p, axis=-1, keepdims=True)


def _xattn_prompt_body(x_ref, mk_ref, mv_ref, g_ref, wq_ref, wo_ref, y_ref):
    x = x_ref[0]
    h = _rms(x, g_ref[...]).astype(BF16)
    q = (_dot(h, wq_ref[...]) * (XA_HD ** -0.5)).astype(BF16)
    outs = []
    for hh in range(XA_HEADS):
        hs = slice(hh * XA_HD, (hh + 1) * XA_HD)
        s = _dot_nt(q[:, hs], mk_ref[0, :, hs].astype(BF16))
        p = _softmax_rows(s).astype(BF16)
        outs.append(_dot(p, mv_ref[0, :, hs].astype(BF16)))
    o = jnp.concatenate(outs, axis=1).astype(BF16)
    y_ref[0] = x + _dot(o, wo_ref[...])


def _xattn_prompt(x, mk, mv, g, wq, wo):
    bsz, seq, _ = x.shape
    tile = XA_TILE
    xspec = pl.BlockSpec((1, tile, D_MODEL), lambda b, t: (b, t, 0))
    mspec = pl.BlockSpec((1, N_MEM, D_MODEL), lambda b, t: (b, 0, 0))
    return pl.pallas_call(
        _xattn_prompt_body,
        grid=(bsz, seq // tile),
        in_specs=[xspec, mspec, mspec, _const_spec((1, D_MODEL)),
                  _const_spec((D_MODEL, D_MODEL)), _const_spec((D_MODEL, D_MODEL))],
        out_specs=xspec,
        out_shape=jax.ShapeDtypeStruct((bsz, seq, D_MODEL), F32),
        compiler_params=_params("arbitrary", "arbitrary"),
        name="xattn_prompt",
    )(x, mk, mv, g, wq, wo)


def _ffn_body(x_ref, g_ref, wup_ref, wdn_ref, gf_ref, y_ref):
    x = x_ref[...]
    h = _rms(x, g_ref[...]).astype(BF16)
    acc = x
    for c in range(D_FF // FFN_CHUNK):
        cs = slice(c * FFN_CHUNK, (c + 1) * FFN_CHUNK)
        a = jnp.square(jnp.maximum(_dot(h, wup_ref[:, cs]), 0.0)).astype(BF16)
        acc = acc + _dot(a, wdn_ref[cs, :])
    y_ref[...] = _rms(acc, gf_ref[...])


def _ffn(x, g, wup, wdn, gf):
    n = x.shape[0]
    tile = min(FFN_TILE, n)
    row = pl.BlockSpec((tile, D_MODEL), lambda i: (i, 0))
    return pl.pallas_call(
        _ffn_body,
        grid=(n // tile,),
        in_specs=[row, _const_spec((1, D_MODEL)), _const_spec((D_MODEL, D_FF)),
                  _const_spec((D_FF, D_MODEL)), _const_spec((1, D_MODEL))],
        out_specs=row,
        out_shape=jax.ShapeDtypeStruct((n, D_MODEL), F32),
        compiler_params=_params("arbitrary"),
        name="ffn",
    )(x, g, wup, wdn, gf)


def _sample_in_body(x_ref, cache_ref, g_ref, win_ref, walpha_ref, balpha_ref, wdw_ref,
                    bdw_ref, lng_ref, lnb_ref,
                    conv_ref, cache_out_ref, a_ref, k_ref, q_ref, v_ref, r_ref):
    x = x_ref[...]
    h = _rms(x, g_ref[...]).astype(BF16)
    proj = _dot(h, win_ref[...])
    u = proj[:, C_VAL:C_VAL + D_CONV] * _sigmoid(proj[:, C_GATE:C_GATE + D_CONV])
    cache = cache_ref[...]
    w = wdw_ref[...]
    y = jnp.sum(cache * w[0:CONV_HIST, :][None], axis=1) + u * w[CONV_HIST:CONV_W, :]
    conv_ref[...] = _silu(_layer_norm(y + bdw_ref[...], lng_ref[...], lnb_ref[...]))
    cache_out_ref[:, 0:CONV_HIST - 1, :] = cache[:, 1:CONV_HIST, :]
    cache_out_ref[:, CONV_HIST - 1:CONV_HIST, :] = u[:, None, :]

    z = _dot(proj[:, C_A:C_A + LANES].astype(BF16), walpha_ref[...]) + balpha_ref[...]
    a_ref[...] = jnp.exp(_log_sigmoid(z) / GLA_TAU)
    k_ref[...] = proj[:, C_K:C_K + GLA_QK]
    q_ref[...] = proj[:, C_Q:C_Q + GLA_QK] * (GLA_DK ** -0.5)
    v_ref[...] = proj[:, C_V:C_V + D_GLA]
    r_ref[...] = proj[:, C_R:C_R + D_GLA]


def _sample_in(x, cache, g, win, walpha, balpha, wdw, bdw, lng, lnb):
    n = x.shape[0]
    bs = SAMPLE_BLOCK

    def rows(width):
        return pl.BlockSpec((bs, width), lambda i: (i, 0))

    cspec = pl.BlockSpec((bs, CONV_HIST, D_CONV), lambda i: (i, 0, 0))
    return pl.pallas_call(
        _sample_in_body,
        grid=(n // bs,),
        in_specs=[rows(D_MODEL), cspec,
                  _const_spec((1, D_MODEL)),
                  _const_spec((D_MODEL, D_IN_PAD)),
                  _const_spec((LANES, GLA_QK)),
                  _const_spec((1, GLA_QK)),
                  _const_spec((CONV_W, D_CONV)),
                  _const_spec((1, D_CONV)),
                  _const_spec((1, D_CONV)),
                  _const_spec((1, D_CONV))],
        out_specs=[rows(D_CONV), cspec, rows(GLA_QK), rows(GLA_QK), rows(GLA_QK),
                   rows(D_GLA), rows(D_GLA)],
        out_shape=[jax.ShapeDtypeStruct((n, D_CONV), F32),
                   jax.ShapeDtypeStruct((n, CONV_HIST, D_CONV), F32),
                   jax.ShapeDtypeStruct((n, GLA_QK), F32),
                   jax.ShapeDtypeStruct((n, GLA_QK), F32),
                   jax.ShapeDtypeStruct((n, GLA_QK), F32),
                   jax.ShapeDtypeStruct((n, D_GLA), F32),
                   jax.ShapeDtypeStruct((n, D_GLA), F32)],
        compiler_params=_params("arbitrary"),
        name="sample_in",
    )(x, cache, g, win, walpha, balpha, wdw, bdw, lng, lnb)


def _sample_state_body(s_ref, at_ref, kt_ref, qt_ref, v_ref, snew_ref, o_ref):
    bs = s_ref.shape[0]
    at = at_ref[0]
    kt = kt_ref[0]
    qt = qt_ref[0]
    for i in range(bs):
        a_col = at[:, i:i + 1]
        k_col = kt[:, i:i + 1]
        q_col = qt[:, i:i + 1]
        for hh in range(GLA_HEADS):
            ds = slice(hh * GLA_DK, (hh + 1) * GLA_DK)
            vs = slice(hh * GLA_DV, (hh + 1) * GLA_DV)
            s_new = a_col[ds, :] * s_ref[i, hh] + k_col[ds, :] * v_ref[i:i + 1, vs]
            snew_ref[i, hh] = s_new
            o_ref[i:i + 1, vs] = jnp.sum(q_col[ds, :] * s_new, axis=0, keepdims=True)


def _sample_state(state, at, kt, qt, v):
    n = state.shape[0]
    bs = STATE_BLOCK
    sspec = pl.BlockSpec((bs, GLA_HEADS, GLA_DK, GLA_DV), lambda i: (i, 0, 0, 0))
    tspec = pl.BlockSpec((1, GLA_QK, bs), lambda i: (i, 0, 0))
    vspec = pl.BlockSpec((bs, D_GLA), lambda i: (i, 0))
    return pl.pallas_call(
        _sample_state_body,
        grid=(n // bs,),
        in_specs=[sspec, tspec, tspec, tspec, vspec],
        out_specs=[sspec, vspec],
        out_shape=[jax.ShapeDtypeStruct(state.shape, F32),
                   jax.ShapeDtypeStruct((n, D_GLA), F32)],
        compiler_params=_params("arbitrary"),
        name="sample_state",
    )(state, at, kt, qt, v)


def _sample_mid_body(x_ref, conv_ref, o_ref, r_ref, glag_ref, wout_ref, gx_ref, wq_ref,
                     x1_ref, q_ref):
    og = _gla_out_norm(o_ref[...], glag_ref[...], r_ref[...])
    mixed = jnp.concatenate([conv_ref[...], og], axis=1).astype(BF16)
    x1 = x_ref[...] + _dot(mixed, wout_ref[...])
    x1_ref[...] = x1
    h = _rms(x1, gx_ref[...]).astype(BF16)
    q = _dot(h, wq_ref[...]) * (XA_HD ** -0.5)
    for row in range(XA_ROWS):
        q_ref[:, row, :] = q[:, _xa_col(row):_xa_col(row) + LANES]


def _sample_mid(x, conv, o, r, glag, wout, gx, wq):
    n = x.shape[0]
    full = lambda w: pl.BlockSpec((n, w), lambda i: (0, 0))
    return pl.pallas_call(
        _sample_mid_body,
        grid=(1,),
        in_specs=[full(D_MODEL), full(D_CONV), full(D_GLA), full(D_GLA),
                  _const_spec((1, GLA_DV)), _const_spec((D_MODEL, D_MODEL)),
                  _const_spec((1, D_MODEL)), _const_spec((D_MODEL, D_MODEL))],
        out_specs=[full(D_MODEL), pl.BlockSpec((n, XA_ROWS, LANES), lambda i: (0, 0, 0))],
        out_shape=[jax.ShapeDtypeStruct((n, D_MODEL), F32),
                   jax.ShapeDtypeStruct((n, XA_ROWS, LANES), F32)],
        compiler_params=_params("arbitrary"),
        name="sample_mid",
    )(x, conv, o, r, glag, wout, gx, wq)


def _xattn_sample_body(q_ref, k_ref, v_ref, o_ref):
    bs = k_ref.shape[0]
    for i in range(bs):
        prod = k_ref[i] * q_ref[i][None]
        prod = prod + pltpu.roll(prod, XA_HEADS, 1)
        s = jnp.sum(prod, axis=-1, keepdims=True)
        e = jnp.exp(s - jnp.max(s, axis=0, keepdims=True))
        denom = jnp.sum(e, axis=0)
        o_ref[i] = jnp.sum(e * v_ref[i], axis=0) / denom


def _memory_rows(m):
    n = m.shape[0]
    m = m.reshape(n, N_MEM, XA_HEADS, XA_LANE_TILES, LANES)
    return m.transpose(0, 1, 3, 2, 4).reshape(n, N_MEM, XA_ROWS, LANES)


def _xattn_sample(q, mk, mv):
    n = mk.shape[0]
    bs = XA_SAMPLE_BLOCK
    qspec = pl.BlockSpec((bs, XA_ROWS, LANES), lambda i: (i, 0, 0))
    mspec = pl.BlockSpec((bs, N_MEM, XA_ROWS, LANES), lambda i: (i, 0, 0, 0))
    return pl.pallas_call(
        _xattn_sample_body,
        grid=(n // bs,),
        in_specs=[qspec, mspec, mspec],
        out_specs=qspec,
        out_shape=jax.ShapeDtypeStruct((n, XA_ROWS, LANES), F32),
        compiler_params=_params("arbitrary"),
        name="xattn_sample",
    )(q, _memory_rows(mk), _memory_rows(mv))


def _sample_xo_body(x_ref, o_ref, wo_ref, y_ref):
    acc = x_ref[...]
    for row in range(XA_ROWS):
        col = _xa_col(row)
        acc = acc + _dot(o_ref[:, row, :].astype(BF16), wo_ref[col:col + LANES, :])
    y_ref[...] = acc


def _sample_xo(x, o, wo):
    n = x.shape[0]
    full = pl.BlockSpec((n, D_MODEL), lambda i: (0, 0))
    return pl.pallas_call(
        _sample_xo_body,
        grid=(1,),
        in_specs=[full, pl.BlockSpec((n, XA_ROWS, LANES), lambda i: (0, 0, 0)),
                  _const_spec((D_MODEL, D_MODEL))],
        out_specs=full,
        out_shape=jax.ShapeDtypeStruct((n, D_MODEL), F32),
        compiler_params=_params("arbitrary"),
        name="sample_xo",
    )(x, o, wo)


def _row(v):
    return v.reshape(1, -1)


def _layer(l, x_prompt, x_sample, cache_conv, state_gla, cache_mem_k, cache_mem_v, mem_prompt,
           norm_mix_g, w_in, w_alpha, b_alpha, w_dw, b_dw, conv_ln_g, conv_ln_b, gla_norm_g,
           w_out, norm_xa_g, mem_norm_g, w_xq, w_xk, w_xv, w_xo, norm_ffn_g, w_up, w_down,
           final_g):
    bsz, seq, _ = x_prompt.shape
    n_s = x_sample.shape[0]

    win = jnp.pad(w_in[l], ((0, 0), (0, D_IN_PAD - D_IN))).astype(BF16)
    walpha = jnp.pad(w_alpha[l], ((0, LANES - GLA_RANK), (0, 0))).astype(BF16)
    balpha = _row(b_alpha[l])
    wout = w_out[l].astype(BF16)
    wq, wk, wv, wo = (w.astype(BF16) for w in (w_xq[l], w_xk[l], w_xv[l], w_xo[l]))
    wup = w_up[l].astype(BF16)
    wdn = w_down[l].astype(BF16)
    gm, gx, gmem, gf_, gfin = (_row(v) for v in (norm_mix_g[l], norm_xa_g[l], mem_norm_g[l],
                                                 norm_ffn_g[l], final_g))
    bdw, lng, lnb, glag = (_row(v) for v in (b_dw[l], conv_ln_g[l], conv_ln_b[l], gla_norm_g[l]))
    wdw = w_dw[l]

    mk, mv = _memkv(mem_prompt.reshape(bsz * N_MEM, D_MODEL), gmem, wk, wv)
    mk = mk.reshape(bsz, N_MEM, D_MODEL)
    mv = mv.reshape(bsz, N_MEM, D_MODEL)
    x1, conv_p, gla_p = _mix_prompt(x_prompt, gm, win, walpha, balpha, wdw, bdw, lng, lnb,
                                    glag, wout)
    x2 = _xattn_prompt(x1, mk, mv, gx, wq, wo)
    y_p = _ffn(x2.reshape(bsz * seq, D_MODEL), gf_, wup, wdn, gfin).reshape(bsz, seq, D_MODEL)

    xs = x_sample.reshape(n_s, D_MODEL)
    conv_s_out, cache_new, a, k, q, v, r = _sample_in(
        xs, cache_conv[l], gm, win, walpha, balpha, wdw, bdw, lng, lnb)
    nb = n_s // STATE_BLOCK

    def cols(m):
        return m.reshape(nb, STATE_BLOCK, GLA_QK).transpose(0, 2, 1)

    state_new, o_s = _sample_state(state_gla[l], cols(a), cols(k), cols(q), v)
    x1s, qx = _sample_mid(xs, conv_s_out, o_s, r, glag, wout, gx, wq)
    oa = _xattn_sample(qx, cache_mem_k[l], cache_mem_v[l])
    x2s = _sample_xo(x1s, oa, wo)
    y_s = _ffn(x2s, gf_, wup, wdn, gfin).reshape(n_s, 1, D_MODEL)

    return (y_p, y_s, conv_p, cache_new,
            gla_p.reshape(bsz, GLA_HEADS, GLA_DK, GLA_DV), state_new,
            mk.reshape(bsz, N_MEM, XA_HEADS, XA_HD), mv.reshape(bsz, N_MEM, XA_HEADS, XA_HD))


def kernel(x_prompt, x_sample, cache_conv, state_gla, cache_mem_k, cache_mem_v, mem_prompt,
           norm_mix_g, w_in, w_alpha, b_alpha, w_dw, b_dw, conv_ln_g, conv_ln_b, gla_norm_g,
           w_out, norm_xa_g, mem_norm_g, w_xq, w_xk, w_xv, w_xo, norm_ffn_g, w_up, w_down,
           final_g):
    assert w_in.shape[0] == 1, "single-layer trunk"
    outs = _layer(0, x_prompt, x_sample, cache_conv, state_gla, cache_mem_k, cache_mem_v,
                  mem_prompt, norm_mix_g, w_in, w_alpha, b_alpha, w_dw, b_dw, conv_ln_g,
                  conv_ln_b, gla_norm_g, w_out, norm_xa_g, mem_norm_g, w_xq, w_xk, w_xv, w_xo,
                  norm_ffn_g, w_up, w_down, final_g)
    y_p, y_s, conv_p, conv_s, gla_p, gla_s, mk, mv = outs
    return (y_p, y_s, conv_p[None], conv_s[None], gla_p[None], gla_s[None], mk[None], mv[None])
```

```python
import functools

import jax
import jax.numpy as jnp
from jax import lax
from jax.experimental import pallas as pl
from jax.experimental.pallas import tpu as pltpu

F32 = jnp.float32
BF16 = jnp.bfloat16

D_MODEL = 1024
D_CONV = 512
D_GLA = 512
CONV_W = 31
CONV_HIST = CONV_W - 1
GLA_HEADS = 4
GLA_DK = 64
GLA_DV = 128
GLA_QK = GLA_HEADS * GLA_DK
GLA_RANK = 16
GLA_TAU = 16.0
N_MEM = 256
XA_HEADS = 4
XA_HD = 256
D_FF = 4096
EPS = 1e-6

LANES = 128
SUBLANES = 8
VMEM_LIMIT_BYTES = 56 * 1024 * 1024

C_VAL = 0
C_GATE = D_CONV
C_Q = 2 * D_CONV
C_K = C_Q + GLA_QK
C_V = C_K + GLA_QK
C_R = C_V + D_GLA
C_A = C_R + D_GLA
D_IN = C_A + GLA_RANK
D_IN_PAD = C_A + LANES

CONV_PAD = 32
CONV_ROW_BLOCK = 64

MIX_TILE = 256
GLA_SAFE_LOG_DECAY = 60.0
XA_TILE = 512
FFN_TILE = 512
FFN_CHUNK = 1024
SAMPLE_BLOCK = 32
STATE_BLOCK = 16
XA_SAMPLE_BLOCK = 8

XA_LANE_TILES = XA_HD // LANES
XA_ROWS = XA_HEADS * XA_LANE_TILES


def _xa_col(row):
    return (row % XA_HEADS) * XA_HD + (row // XA_HEADS) * LANES


def _const_spec(shape):
    n = len(shape)
    return pl.BlockSpec(shape, lambda *_: (0,) * n, pipeline_mode=pl.Buffered(1))


def _params(*sem):
    return pltpu.CompilerParams(dimension_semantics=sem, vmem_limit_bytes=VMEM_LIMIT_BYTES)


def _rms(x, g):
    return x * lax.rsqrt(jnp.mean(x * x, axis=-1, keepdims=True) + EPS) * g


def _layer_norm(x, g, b):
    mu = jnp.mean(x, axis=-1, keepdims=True)
    xc = x - mu
    return xc * lax.rsqrt(jnp.mean(xc * xc, axis=-1, keepdims=True) + EPS) * g + b


def _sigmoid(x):
    return 1.0 / (1.0 + jnp.exp(-x))


def _silu(x):
    return x * _sigmoid(x)


def _log_sigmoid(x):
    return jnp.minimum(x, 0.0) - jnp.log(1.0 + jnp.exp(-jnp.abs(x)))


def _dot(a, b):
    return jnp.dot(a, b, preferred_element_type=F32)


def _dot_nt(a, b):
    return lax.dot_general(a, b, (((1,), (1,)), ((), ())), preferred_element_type=F32)


def _dot_tn(a, b):
    return lax.dot_general(a, b, (((0,), (0,)), ((), ())), preferred_element_type=F32)


def _head_mask(width, per_head, h):
    lane = lax.broadcasted_iota(jnp.int32, (1, width), 1)
    return (lane >= h * per_head) & (lane < (h + 1) * per_head)


def _gla_out_norm(o, gla_g, r):
    parts = []
    for h in range(GLA_HEADS):
        parts.append(_rms(o[:, h * GLA_DV:(h + 1) * GLA_DV], gla_g))
    return jnp.concatenate(parts, axis=1) * _silu(r)


def _split3(x):
    hi = x.astype(BF16)
    rem = x - hi.astype(F32)
    mid = rem.astype(BF16)
    lo = (rem - mid.astype(F32)).astype(BF16)
    return jnp.concatenate([hi, mid, lo], axis=1)


def _sum3(y):
    w = y.shape[1] // 3
    return y[:, 0:w] + y[:, w:2 * w] + y[:, 2 * w:3 * w]


def _per_head_rows(a):
    return jnp.concatenate(
        [jnp.where(_head_mask(GLA_QK, GLA_DK, hh), a, 0.0) for hh in range(GLA_HEADS)],
        axis=0).astype(BF16)


def _gla_scores_bisect(q, k, la3):
    n = q.shape[0]
    row4 = lax.broadcasted_iota(jnp.int32, (GLA_HEADS * n, n), 0) & (n - 1)
    col4 = lax.broadcasted_iota(jnp.int32, (GLA_HEADS * n, n), 1)
    row = lax.broadcasted_iota(jnp.int32, (n, n), 0)
    col = lax.broadcasted_iota(jnp.int32, (n, n), 1)
    rid = lax.broadcasted_iota(jnp.int32, (n, 1), 0)
    total = jnp.where(row4 == col4, _dot_nt(_per_head_rows(q), k.astype(BF16)), 0.0)
    for lvl in range(n.bit_length() - 1):
        s = 1 << lvl
        pos = row & (2 * s - 1)
        anchor = row - pos + (s - 1)
        right = pos >= s
        dmat = jnp.where(right & (col > anchor) & (col <= row), 1.0,
                         jnp.where((~right) & (col > row) & (col <= anchor), -1.0, 0.0))
        decay = jnp.exp(-jnp.abs(_sum3(_dot(dmat.astype(BF16), la3))))
        right_row = (rid & (2 * s - 1)) >= s
        qt = jnp.where(right_row, q * decay, 0.0)
        kt = jnp.where(right_row, 0.0, k * decay)
        sc = _dot_nt(_per_head_rows(qt), kt.astype(BF16))
        total = total + jnp.where((row4 >> (lvl + 1)) == (col4 >> (lvl + 1)), sc, 0.0)
    return total


def _memkv_body(m_ref, g_ref, wk_ref, wv_ref, k_ref, v_ref):
    h = _rms(m_ref[...], g_ref[...]).astype(BF16)
    k_ref[...] = _dot(h, wk_ref[...])
    v_ref[...] = _dot(h, wv_ref[...])


def _memkv(mem, g, wk, wv):
    n = mem.shape[0]
    tile = 512
    row = pl.BlockSpec((tile, D_MODEL), lambda i: (i, 0))
    return pl.pallas_call(
        _memkv_body,
        grid=(n // tile,),
        in_specs=[row, _const_spec((1, D_MODEL)), _const_spec((D_MODEL, D_MODEL)),
                  _const_spec((D_MODEL, D_MODEL))],
        out_specs=[row, row],
        out_shape=[jax.ShapeDtypeStruct((n, D_MODEL), F32)] * 2,
        compiler_params=_params("arbitrary"),
        name="memkv",
    )(mem, g, wk, wv)


def _mix_prompt_body(x_ref, g_ref, win_ref, walpha_ref, balpha_ref, wdw_ref, bdw_ref,
                     lng_ref, lnb_ref, glag_ref, wout_ref,
                     y_ref, conv_ref, gla_ref,
                     ubuf, st_ref, conv_out, shifted, sc_ref):
    t = pl.program_id(1)
    n_t = pl.num_programs(1)
    tile = x_ref.shape[1]

    @pl.when(t == 0)
    def _():
        ubuf[0:CONV_PAD, :] = jnp.zeros((CONV_PAD, D_CONV), F32)
        st_ref[...] = jnp.zeros_like(st_ref)

    x = x_ref[0]
    h = _rms(x, g_ref[...]).astype(BF16)
    proj = _dot(h, win_ref[...])

    u = proj[:, C_VAL:C_VAL + D_CONV] * _sigmoid(proj[:, C_GATE:C_GATE + D_CONV])
    ubuf[CONV_PAD:CONV_PAD + tile, :] = u
    n_shift = tile + CONV_PAD - SUBLANES
    for s in range(1, SUBLANES):
        shifted[s - 1, 0:n_shift, :] = ubuf[s:s + n_shift, :]
    first = CONV_PAD - CONV_HIST
    for c in range(D_CONV // LANES):
        cs = slice(c * LANES, (c + 1) * LANES)
        for r in range(tile // CONV_ROW_BLOCK):
            acc = jnp.zeros((CONV_ROW_BLOCK, LANES), F32)
            for k in range(CONV_W):
                s = (first + k) % SUBLANES
                lo = first + k - s + r * CONV_ROW_BLOCK
                src = ubuf if s == 0 else shifted.at[s - 1]
                acc = acc + src[lo:lo + CONV_ROW_BLOCK, cs] * wdw_ref[k:k + 1, cs]
            conv_out[r * CONV_ROW_BLOCK:(r + 1) * CONV_ROW_BLOCK, cs] = acc
    conv = _silu(_layer_norm(conv_out[...] + bdw_ref[...], lng_ref[...], lnb_ref[...]))

    @pl.when(t == n_t - 1)
    def _():
        conv_ref[0] = ubuf[CONV_PAD + tile - CONV_HIST:CONV_PAD + tile, :]

    ubuf[0:CONV_PAD, :] = ubuf[tile:tile + CONV_PAD, :]

    z = _dot(proj[:, C_A:C_A + LANES].astype(BF16), walpha_ref[...]) + balpha_ref[...]
    la = _log_sigmoid(z) / GLA_TAU
    la3 = _split3(la)
    row = lax.broadcasted_iota(jnp.int32, (tile, tile), 0)
    col = lax.broadcasted_iota(jnp.int32, (tile, tile), 1)
    causal = row >= col
    b = _sum3(_dot(jnp.where(causal, 1.0, 0.0).astype(BF16), la3))
    b_last = b[tile - 1:tile, :]

    q = proj[:, C_Q:C_Q + GLA_QK] * (GLA_DK ** -0.5)
    k = proj[:, C_K:C_K + GLA_QK]
    v = proj[:, C_V:C_V + D_GLA].astype(BF16)
    qe4 = _per_head_rows(q * jnp.exp(b))

    safe = jnp.min(b_last) > -GLA_SAFE_LOG_DECAY

    @pl.when(safe)
    def _():
        sc_ref[...] = _dot_nt(qe4, (k * jnp.exp(-b)).astype(BF16))

    @pl.when(jnp.logical_not(safe))
    def _():
        sc_ref[...] = _gla_scores_bisect(q, k, la3)

    kl = (k * jnp.exp(b_last - b)).astype(BF16)
    st = st_ref[...]
    inter = _dot_nt(qe4, st.astype(BF16))
    outs = []
    for hh in range(GLA_HEADS):
        p = jnp.where(causal, sc_ref[hh * tile:(hh + 1) * tile, :], 0.0).astype(BF16)
        outs.append(_dot(p, v[:, hh * GLA_DV:(hh + 1) * GLA_DV])
                    + inter[hh * tile:(hh + 1) * tile, :])
    o = jnp.concatenate(outs, axis=1)

    upd = _dot_tn(v, kl)
    new_st = st * jnp.exp(b_last)
    for hh in range(GLA_HEADS):
        new_st = new_st + jnp.where(_head_mask(GLA_QK, GLA_DK, hh),
                                    upd[hh * GLA_DV:(hh + 1) * GLA_DV, :], 0.0)
    st_ref[...] = new_st

    @pl.when(t == n_t - 1)
    def _():
        gla_ref[0] = jnp.transpose(new_st)

    og = _gla_out_norm(o, glag_ref[...], proj[:, C_R:C_R + D_GLA])
    mixed = jnp.concatenate([conv, og], axis=1).astype(BF16)
    y_ref[0] = x + _dot(mixed, wout_ref[...])


def _mix_prompt(x, g, win, walpha, balpha, wdw, bdw, lng, lnb, glag, wout):
    bsz, seq, _ = x.shape
    tile = MIX_TILE
    xspec = pl.BlockSpec((1, tile, D_MODEL), lambda b, t: (b, t, 0))
    return pl.pallas_call(
        _mix_prompt_body,
        grid=(bsz, seq // tile),
        in_specs=[xspec,
                  _const_spec((1, D_MODEL)),
                  _const_spec((D_MODEL, D_IN_PAD)),
                  _const_spec((LANES, GLA_QK)),
                  _const_spec((1, GLA_QK)),
                  _const_spec((CONV_W, D_CONV)),
                  _const_spec((1, D_CONV)),
                  _const_spec((1, D_CONV)),
                  _const_spec((1, D_CONV)),
                  _const_spec((1, GLA_DV)),
                  _const_spec((D_MODEL, D_MODEL))],
        out_specs=[xspec,
                   pl.BlockSpec((1, CONV_HIST, D_CONV), lambda b, t: (b, 0, 0)),
                   pl.BlockSpec((1, GLA_QK, GLA_DV), lambda b, t: (b, 0, 0))],
        out_shape=[jax.ShapeDtypeStruct((bsz, seq, D_MODEL), F32),
                   jax.ShapeDtypeStruct((bsz, CONV_HIST, D_CONV), F32),
                   jax.ShapeDtypeStruct((bsz, GLA_QK, GLA_DV), F32)],
        scratch_shapes=[pltpu.VMEM((CONV_PAD + tile, D_CONV), F32),
                        pltpu.VMEM((GLA_DV, GLA_QK), F32),
                        pltpu.VMEM((tile, D_CONV), F32),
                        pltpu.VMEM((SUBLANES - 1, CONV_PAD + tile, D_CONV), F32),
                        pltpu.VMEM((GLA_HEADS * tile, tile), F32)],
        compiler_params=_params("arbitrary", "arbitrary"),
        name="mix_prompt",
    )(x, g, win, walpha, balpha, wdw, bdw, lng, lnb, glag, wout)


def _softmax_rows(s):
    m = jnp.max(s, axis=-1, keepdims=True)
    p = jnp.exp(s - m)
    return p / jnp.sum(p, axis=-1, keepdims=True)


def _xattn_prompt_body(x_ref, mk_ref, mv_ref, g_ref, wq_ref, wo_ref, y_ref):
    x = x_ref[0]
    h = _rms(x, g_ref[...]).astype(BF16)
    q = (_dot(h, wq_ref[...]) * (XA_HD ** -0.5)).astype(BF16)
    outs = []
    for hh in range(XA_HEADS):
        hs = slice(hh * XA_HD, (hh + 1) * XA_HD)
        s = _dot_nt(q[:, hs], mk_ref[0, :, hs].astype(BF16))
        p = _softmax_rows(s).astype(BF16)
        outs.append(_dot(p, mv_ref[0, :, hs].astype(BF16)))
    o = jnp.concatenate(outs, axis=1).astype(BF16)
    y_ref[0] = x + _dot(o, wo_ref[...])


def _xattn_prompt(x, mk, mv, g, wq, wo):
    bsz, seq, _ = x.shape
    tile = XA_TILE
    xspec = pl.BlockSpec((1, tile, D_MODEL), lambda b, t: (b, t, 0))
    mspec = pl.BlockSpec((1, N_MEM, D_MODEL), lambda b, t: (b, 0, 0))
    return pl.pallas_call(
        _xattn_prompt_body,
        grid=(bsz, seq // tile),
        in_specs=[xspec, mspec, mspec, _const_spec((1, D_MODEL)),
                  _const_spec((D_MODEL, D_MODEL)), _const_spec((D_MODEL, D_MODEL))],
        out_specs=xspec,
        out_shape=jax.ShapeDtypeStruct((bsz, seq, D_MODEL), F32),
        compiler_params=_params("arbitrary", "arbitrary"),
        name="xattn_prompt",
    )(x, mk, mv, g, wq, wo)


def _ffn_body(x_ref, g_ref, wup_ref, wdn_ref, gf_ref, y_ref):
    x = x_ref[...]
    h = _rms(x, g_ref[...]).astype(BF16)
    acc = x
    for c in range(D_FF // FFN_CHUNK):
        cs = slice(c * FFN_CHUNK, (c + 1) * FFN_CHUNK)
        a = jnp.square(jnp.maximum(_dot(h, wup_ref[:, cs]), 0.0)).astype(BF16)
        acc = acc + _dot(a, wdn_ref[cs, :])
    y_ref[...] = _rms(acc, gf_ref[...])


def _ffn(x, g, wup, wdn, gf):
    n = x.shape[0]
    tile = min(FFN_TILE, n)
    row = pl.BlockSpec((tile, D_MODEL), lambda i: (i, 0))
    return pl.pallas_call(
        _ffn_body,
        grid=(n // tile,),
        in_specs=[row, _const_spec((1, D_MODEL)), _const_spec((D_MODEL, D_FF)),
                  _const_spec((D_FF, D_MODEL)), _const_spec((1, D_MODEL))],
        out_specs=row,
        out_shape=jax.ShapeDtypeStruct((n, D_MODEL), F32),
        compiler_params=_params("arbitrary"),
        name="ffn",
    )(x, g, wup, wdn, gf)


def _sample_in_body(x_ref, cache_ref, g_ref, win_ref, walpha_ref, balpha_ref, wdw_ref,
                    bdw_ref, lng_ref, lnb_ref,
                    conv_ref, cache_out_ref, a_ref, k_ref, q_ref, v_ref, r_ref):
    x = x_ref[...]
    h = _rms(x, g_ref[...]).astype(BF16)
    proj = _dot(h, win_ref[...])
    u = proj[:, C_VAL:C_VAL + D_CONV] * _sigmoid(proj[:, C_GATE:C_GATE + D_CONV])
    cache = cache_ref[...]
    w = wdw_ref[...]
    y = jnp.sum(cache * w[0:CONV_HIST, :][None], axis=1) + u * w[CONV_HIST:CONV_W, :]
    conv_ref[...] = _silu(_layer_norm(y + bdw_ref[...], lng_ref[...], lnb_ref[...]))
    cache_out_ref[:, 0:CONV_HIST - 1, :] = cache[:, 1:CONV_HIST, :]
    cache_out_ref[:, CONV_HIST - 1:CONV_HIST, :] = u[:, None, :]

    z = _dot(proj[:, C_A:C_A + LANES].astype(BF16), walpha_ref[...]) + balpha_ref[...]
    a_ref[...] = jnp.exp(_log_sigmoid(z) / GLA_TAU)
    k_ref[...] = proj[:, C_K:C_K + GLA_QK]
    q_ref[...] = proj[:, C_Q:C_Q + GLA_QK] * (GLA_DK ** -0.5)
    v_ref[...] = proj[:, C_V:C_V + D_GLA]
    r_ref[...] = proj[:, C_R:C_R + D_GLA]


def _sample_in(x, cache, g, win, walpha, balpha, wdw, bdw, lng, lnb):
    n = x.shape[0]
    bs = SAMPLE_BLOCK

    def rows(width):
        return pl.BlockSpec((bs, width), lambda i: (i, 0))

    cspec = pl.BlockSpec((bs, CONV_HIST, D_CONV), lambda i: (i, 0, 0))
    return pl.pallas_call(
        _sample_in_body,
        grid=(n // bs,),
        in_specs=[rows(D_MODEL), cspec,
                  _const_spec((1, D_MODEL)),
                  _const_spec((D_MODEL, D_IN_PAD)),
                  _const_spec((LANES, GLA_QK)),
                  _const_spec((1, GLA_QK)),
                  _const_spec((CONV_W, D_CONV)),
                  _const_spec((1, D_CONV)),
                  _const_spec((1, D_CONV)),
                  _const_spec((1, D_CONV))],
        out_specs=[rows(D_CONV), cspec, rows(GLA_QK), rows(GLA_QK), rows(GLA_QK),
                   rows(D_GLA), rows(D_GLA)],
        out_shape=[jax.ShapeDtypeStruct((n, D_CONV), F32),
                   jax.ShapeDtypeStruct((n, CONV_HIST, D_CONV), F32),
                   jax.ShapeDtypeStruct((n, GLA_QK), F32),
                   jax.ShapeDtypeStruct((n, GLA_QK), F32),
                   jax.ShapeDtypeStruct((n, GLA_QK), F32),
                   jax.ShapeDtypeStruct((n, D_GLA), F32),
                   jax.ShapeDtypeStruct((n, D_GLA), F32)],
        compiler_params=_params("arbitrary"),
        name="sample_in",
    )(x, cache, g, win, walpha, balpha, wdw, bdw, lng, lnb)


def _sample_state_body(s_ref, at_ref, kt_ref, qt_ref, v_ref, snew_ref, o_ref):
    bs = s_ref.shape[0]
    at = at_ref[0]
    kt = kt_ref[0]
    qt = qt_ref[0]
    for i in range(bs):
        a_col = at[:, i:i + 1]
        k_col = kt[:, i:i + 1]
        q_col = qt[:, i:i + 1]
        for hh in range(GLA_HEADS):
            ds = slice(hh * GLA_DK, (hh + 1) * GLA_DK)
            vs = slice(hh * GLA_DV, (hh + 1) * GLA_DV)
            s_new = a_col[ds, :] * s_ref[i, hh] + k_col[ds, :] * v_ref[i:i + 1, vs]
            snew_ref[i, hh] = s_new
            o_ref[i:i + 1, vs] = jnp.sum(q_col[ds, :] * s_new, axis=0, keepdims=True)


def _sample_state(state, at, kt, qt, v):
    n = state.shape[0]
    bs = STATE_BLOCK
    sspec = pl.BlockSpec((bs, GLA_HEADS, GLA_DK, GLA_DV), lambda i: (i, 0, 0, 0))
    tspec = pl.BlockSpec((1, GLA_QK, bs), lambda i: (i, 0, 0))
    vspec = pl.BlockSpec((bs, D_GLA), lambda i: (i, 0))
    return pl.pallas_call(
        _sample_state_body,
        grid=(n // bs,),
        in_specs=[sspec, tspec, tspec, tspec, vspec],
        out_specs=[sspec, vspec],
        out_shape=[jax.ShapeDtypeStruct(state.shape, F32),
                   jax.ShapeDtypeStruct((n, D_GLA), F32)],
        compiler_params=_params("arbitrary"),
        name="sample_state",
    )(state, at, kt, qt, v)


def _sample_mid_body(x_ref, conv_ref, o_ref, r_ref, glag_ref, wout_ref, gx_ref, wq_ref,
                     x1_ref, q_ref):
    og = _gla_out_norm(o_ref[...], glag_ref[...], r_ref[...])
    mixed = jnp.concatenate([conv_ref[...], og], axis=1).astype(BF16)
    x1 = x_ref[...] + _dot(mixed, wout_ref[...])
    x1_ref[...] = x1
    h = _rms(x1, gx_ref[...]).astype(BF16)
    q = _dot(h, wq_ref[...]) * (XA_HD ** -0.5)
    for row in range(XA_ROWS):
        q_ref[:, row, :] = q[:, _xa_col(row):_xa_col(row) + LANES]


def _sample_mid(x, conv, o, r, glag, wout, gx, wq):
    n = x.shape[0]
    full = lambda w: pl.BlockSpec((n, w), lambda i: (0, 0))
    return pl.pallas_call(
        _sample_mid_body,
        grid=(1,),
        in_specs=[full(D_MODEL), full(D_CONV), full(D_GLA), full(D_GLA),
                  _const_spec((1, GLA_DV)), _const_spec((D_MODEL, D_MODEL)),
                  _const_spec((1, D_MODEL)), _const_spec((D_MODEL, D_MODEL))],
        out_specs=[full(D_MODEL), pl.BlockSpec((n, XA_ROWS, LANES), lambda i: (0, 0, 0))],
        out_shape=[jax.ShapeDtypeStruct((n, D_MODEL), F32),
                   jax.ShapeDtypeStruct((n, XA_ROWS, LANES), F32)],
        compiler_params=_params("arbitrary"),
        name="sample_mid",
    )(x, conv, o, r, glag, wout, gx, wq)


def _xattn_sample_body(q_ref, k_ref, v_ref, o_ref):
    bs = k_ref.shape[0]
    for i in range(bs):
        prod = k_ref[i] * q_ref[i][None]
        prod = prod + pltpu.roll(prod, XA_HEADS, 1)
        s = jnp.sum(prod, axis=-1, keepdims=True)
        e = jnp.exp(s - jnp.max(s, axis=0, keepdims=True))
        denom = jnp.sum(e, axis=0)
        o_ref[i] = jnp.sum(e * v_ref[i], axis=0) / denom


def _memory_rows(m):
    n = m.shape[0]
    m = m.reshape(n, N_MEM, XA_HEADS, XA_LANE_TILES, LANES)
    return m.transpose(0, 1, 3, 2, 4).reshape(n, N_MEM, XA_ROWS, LANES)


def _xattn_sample(q, mk, mv):
    n = mk.shape[0]
    bs = XA_SAMPLE_BLOCK
    qspec = pl.BlockSpec((bs, XA_ROWS, LANES), lambda i: (i, 0, 0))
    mspec = pl.BlockSpec((bs, N_MEM, XA_ROWS, LANES), lambda i: (i, 0, 0, 0))
    return pl.pallas_call(
        _xattn_sample_body,
        grid=(n // bs,),
        in_specs=[qspec, mspec, mspec],
        out_specs=qspec,
        out_shape=jax.ShapeDtypeStruct((n, XA_ROWS, LANES), F32),
        compiler_params=_params("arbitrary"),
        name="xattn_sample",
    )(q, _memory_rows(mk), _memory_rows(mv))


def _sample_xo_body(x_ref, o_ref, wo_ref, y_ref):
    acc = x_ref[...]
    for row in range(XA_ROWS):
        col = _xa_col(row)
        acc = acc + _dot(o_ref[:, row, :].astype(BF16), wo_ref[col:col + LANES, :])
    y_ref[...] = acc


def _sample_xo(x, o, wo):
    n = x.shape[0]
    full = pl.BlockSpec((n, D_MODEL), lambda i: (0, 0))
    return pl.pallas_call(
        _sample_xo_body,
        grid=(1,),
        in_specs=[full, pl.BlockSpec((n, XA_ROWS, LANES), lambda i: (0, 0, 0)),
                  _const_spec((D_MODEL, D_MODEL))],
        out_specs=full,
        out_shape=jax.ShapeDtypeStruct((n, D_MODEL), F32),
        compiler_params=_params("arbitrary"),
        name="sample_xo",
    )(x, o, wo)


def _row(v):
    return v.reshape(1, -1)


def _layer(l, x_prompt, x_sample, cache_conv, state_gla, cache_mem_k, cache_mem_v, mem_prompt,
           norm_mix_g, w_in, w_alpha, b_alpha, w_dw, b_dw, conv_ln_g, conv_ln_b, gla_norm_g,
           w_out, norm_xa_g, mem_norm_g, w_xq, w_xk, w_xv, w_xo, norm_ffn_g, w_up, w_down,
           final_g):
    bsz, seq, _ = x_prompt.shape
    n_s = x_sample.shape[0]

    win = jnp.pad(w_in[l], ((0, 0), (0, D_IN_PAD - D_IN))).astype(BF16)
    walpha = jnp.pad(w_alpha[l], ((0, LANES - GLA_RANK), (0, 0))).astype(BF16)
    balpha = _row(b_alpha[l])
    wout = w_out[l].astype(BF16)
    wq, wk, wv, wo = (w.astype(BF16) for w in (w_xq[l], w_xk[l], w_xv[l], w_xo[l]))
    wup = w_up[l].astype(BF16)
    wdn = w_down[l].astype(BF16)
    gm, gx, gmem, gf_, gfin = (_row(v) for v in (norm_mix_g[l], norm_xa_g[l], mem_norm_g[l],
                                                 norm_ffn_g[l], final_g))
    bdw, lng, lnb, glag = (_row(v) for v in (b_dw[l], conv_ln_g[l], conv_ln_b[l], gla_norm_g[l]))
    wdw = w_dw[l]

    mk, mv = _memkv(mem_prompt.reshape(bsz * N_MEM, D_MODEL), gmem, wk, wv)
    mk = mk.reshape(bsz, N_MEM, D_MODEL)
    mv = mv.reshape(bsz, N_MEM, D_MODEL)
    x1, conv_p, gla_p = _mix_prompt(x_prompt, gm, win, walpha, balpha, wdw, bdw, lng, lnb,
                                    glag, wout)
    x2 = _xattn_prompt(x1, mk, mv, gx, wq, wo)
    y_p = _ffn(x2.reshape(bsz * seq, D_MODEL), gf_, wup, wdn, gfin).reshape(bsz, seq, D_MODEL)

    xs = x_sample.reshape(n_s, D_MODEL)
    conv_s_out, cache_new, a, k, q, v, r = _sample_in(
        xs, cache_conv[l], gm, win, walpha, balpha, wdw, bdw, lng, lnb)
    nb = n_s // STATE_BLOCK

    def cols(m):
        return m.reshape(nb, STATE_BLOCK, GLA_QK).transpose(0, 2, 1)

    state_new, o_s = _sample_state(state_gla[l], cols(a), cols(k), cols(q), v)
    x1s, qx = _sample_mid(xs, conv_s_out, o_s, r, glag, wout, gx, wq)
    oa = _xattn_sample(qx, cache_mem_k[l], cache_mem_v[l])
    x2s = _sample_xo(x1s, oa, wo)
    y_s = _ffn(x2s, gf_, wup, wdn, gfin).reshape(n_s, 1, D_MODEL)

    return (y_p, y_s, conv_p, cache_new,
            gla_p.reshape(bsz, GLA_HEADS, GLA_DK, GLA_DV), state_new,
            mk.reshape(bsz, N_MEM, XA_HEADS, XA_HD), mv.reshape(bsz, N_MEM, XA_HEADS, XA_HD))


def kernel(x_prompt, x_sample, cache_conv, state_gla, cache_mem_k, cache_mem_v, mem_prompt,
           norm_mix_g, w_in, w_alpha, b_alpha, w_dw, b_dw, conv_ln_g, conv_ln_b, gla_norm_g,
           w_out, norm_xa_g, mem_norm_g, w_xq, w_xk, w_xv, w_xo, norm_ffn_g, w_up, w_down,
           final_g):
    assert w_in.shape[0] == 1, "single-layer trunk"
    outs = _layer(0, x_prompt, x_sample, cache_conv, state_gla, cache_mem_k, cache_mem_v,
                  mem_prompt, norm_mix_g, w_in, w_alpha, b_alpha, w_dw, b_dw, conv_ln_g,
                  conv_ln_b, gla_norm_g, w_out, norm_xa_g, mem_norm_g, w_xq, w_xk, w_xv, w_xo,
                  norm_ffn_g, w_up, w_down, final_g)
    y_p, y_s, conv_p, conv_s, gla_p, gla_s, mk, mv = outs
    return (y_p, y_s, conv_p[None], conv_s[None], gla_p[None], gla_s[None], mk[None], mv[None])
```
